```python
import jax, jax.numpy as jnp
from jax import lax
import numpy as np

D_MODEL = 4096
BATCH = 4
SEQ = 2048
DEPTH = 2

D_MIX = D_MODEL
N_MIXERS = 4
GROUP_W = D_MIX // N_MIXERS
HEAD_DIM = 128
N_HEADS = GROUP_W // HEAD_DIM
CHUNK = 128
CONV_W = 4
LRU_C = 8.0
ROPE_BASE = 10000.0
N_EXPERTS = 64
TOP_K = 8
D_EXPERT = D_MODEL // 16
ROUTED_SCALE = 2.5
N_MOD = 6
EPS = 1e-6
N_IN_BLOCKS = 11
N_IN = N_IN_BLOCKS * GROUP_W + N_HEADS

kernel_name = "hybrid_parallel_heads_moe_adaln"


def rms_norm(x, g):
    xf = x.astype(jnp.float32)
    y = xf * lax.rsqrt(jnp.mean(xf * xf, axis=-1, keepdims=True) + EPS)
    return (y * g).astype(x.dtype)


def rms_normalize(x):
    xf = x.astype(jnp.float32)
    return (xf * lax.rsqrt(jnp.mean(xf * xf, axis=-1, keepdims=True) + EPS)).astype(x.dtype)


def heads(t):
    return t.reshape(t.shape[:-1] + (N_HEADS, HEAD_DIM))


def rotary(t, pos):
    half = HEAD_DIM // 2
    inv = ROPE_BASE ** (-jnp.arange(half, dtype=jnp.float32) / half)
    ang = pos[:, None] * inv[None, :]
    cos = jnp.cos(ang)[:, None, :].astype(t.dtype)
    sin = jnp.sin(ang)[:, None, :].astype(t.dtype)
    t1, t2 = t[..., :half], t[..., half:]
    return jnp.concatenate([t1 * cos - t2 * sin, t2 * cos + t1 * sin], axis=-1)


def retention(q, k, v, g):
    B, S, _ = q.shape
    n = S // CHUNK
    pos = jnp.arange(S, dtype=jnp.float32)
    q = rotary(heads(q), pos)
    k = rotary(heads(k), pos) * (HEAD_DIM ** -0.5)
    v = heads(v)
    chunks = lambda t: t.reshape(B, n, CHUNK, N_HEADS, HEAD_DIM).transpose(0, 3, 1, 2, 4)
    qc, kc, vc = chunks(q), chunks(k), chunks(v)
    log_gamma = jnp.log1p(-2.0 ** (-5.0 - jnp.arange(N_HEADS, dtype=jnp.float32)))
    idx = jnp.arange(CHUNK, dtype=jnp.float32)
    rel = idx[:, None] - idx[None, :]
    decay_intra = jnp.exp(jnp.where(rel >= 0, rel[None] * log_gamma[:, None, None], -jnp.inf))
    scores = jnp.einsum('bhncd,bhnmd->bhncm', qc, kc) * decay_intra[:, None]
    intra = jnp.einsum('bhncm,bhnme->bhnce', scores, vc)
    k_decay = jnp.exp((CHUNK - 1 - idx)[None, :] * log_gamma[:, None])
    q_decay = jnp.exp((idx + 1)[None, :] * log_gamma[:, None])
    chunk_kv = jnp.einsum('bhnmd,bhnme->nbhde', kc * k_decay[:, None, :, None], vc)
    chunk_decay = jnp.exp(CHUNK * log_gamma)[:, None, None]

    def step(state, kv):
        return chunk_decay * state + kv, state

    _, prev = lax.scan(step, jnp.zeros_like(chunk_kv[0]), chunk_kv)
    inter = jnp.einsum('bhncd,nbhde->bhnce', qc * q_decay[:, None, :, None], prev)
    y = (intra + inter).astype(q.dtype).transpose(0, 2, 3, 1, 4).reshape(B, S, N_HEADS, HEAD_DIM)
    y = rms_normalize(y).reshape(B, S, GROUP_W)
    return jax.nn.silu(g) * y


def rg_lru_branch(gate_in, x_in, conv_w, conv_b, wa, ba, wx, bx, lam):
    B, S, _ = x_in.shape
    xc = lax.conv_general_dilated(x_in, conv_w[:, None, :], window_strides=(1,),
                                  padding=[(CONV_W - 1, 0)], dimension_numbers=('NWC', 'WIO', 'NWC'),
                                  feature_group_count=GROUP_W) + conv_b
    xg = xc.reshape(B, S, N_HEADS, HEAD_DIM)
    r = jax.nn.sigmoid(jnp.einsum('bsgi,gij->bsgj', xg, wa).reshape(B, S, GROUP_W) + ba)
    i = jax.nn.sigmoid(jnp.einsum('bsgi,gij->bsgj', xg, wx).reshape(B, S, GROUP_W) + bx)
    log_a = (-LRU_C * r * jax.nn.softplus(-lam)).astype(jnp.float32)
    a = jnp.exp(log_a)
    b = jnp.sqrt(-jnp.expm1(2.0 * log_a)) * (i * xc).astype(jnp.float32)

    def combine(left, right):
        a1, b1 = left
        a2, b2 = right
        return a1 * a2, a2 * b1 + b2

    _, h = lax.associative_scan(combine, (a, b), axis=1)
    return h.astype(x_in.dtype) * jax.nn.gelu(gate_in)


def chunked_sgu(u, v, norm_g, w_s, b_s):
    B, S, _ = u.shape
    n = S // CHUNK
    u = jax.nn.gelu(u)
    v = rms_norm(jax.nn.gelu(v), norm_g)
    vc = v.reshape(B, n, CHUNK, N_HEADS, HEAD_DIM)
    mask = jnp.tril(jnp.ones((CHUNK, CHUNK), dtype=bool))
    w = jnp.where(mask[None], w_s, 0.0)
    mixed = jnp.einsum('gts,bnsgc->bntgc', w, vc) + b_s.T[:, :, None]
    return u * mixed.reshape(B, S, GROUP_W)


def forgetting_attention(q, k, v, f_logit, qn, kn, fb):
    B, S, _ = q.shape
    n = S // CHUNK
    q = rms_norm(heads(q), qn).transpose(0, 2, 1, 3)
    k = rms_norm(heads(k), kn).transpose(0, 2, 1, 3)
    v = heads(v).transpose(0, 2, 1, 3)
    log_f = jax.nn.log_sigmoid((f_logit + fb).astype(jnp.float32))
    cum = jnp.cumsum(log_f, axis=1).transpose(0, 2, 1)
    scale = HEAD_DIM ** -0.5
    outs = []
    for blk in range(n):
        q0, q1 = blk * CHUNK, (blk + 1) * CHUNK
        logits = jnp.einsum('bhqd,bhkd->bhqk', q[:, :, q0:q1], k[:, :, :q1]).astype(jnp.float32) * scale
        logits = logits + cum[:, :, q0:q1, None] - cum[:, :, None, :q1]
        causal = (q0 + jnp.arange(CHUNK))[:, None] >= jnp.arange(q1)[None, :]
        p = jax.nn.softmax(jnp.where(causal, logits, -jnp.inf), axis=-1).astype(v.dtype)
        outs.append(jnp.einsum('bhqk,bhkd->bhqd', p, v[:, :, :q1]))
    o = jnp.concatenate(outs, axis=2)
    return o.transpose(0, 2, 1, 3).reshape(B, S, GROUP_W)


def hybrid_mixer(h, w_in, w_out, conv_w, conv_b, wa, ba, wx, bx, lam, sgu_g, sgu_w, sgu_b, qn, kn, fb):
    proj = h @ w_in
    (rq, rk, rv, rg, lg, lx, su, sv, fq, fk, fv, ff) = jnp.split(
        proj, [GROUP_W * i for i in range(1, N_IN_BLOCKS + 1)], axis=-1)
    y = jnp.concatenate([
        retention(rq, rk, rv, rg),
        rg_lru_branch(lg, lx, conv_w, conv_b, wa, ba, wx, bx, lam),
        chunked_sgu(su, sv, sgu_g, sgu_w, sgu_b),
        forgetting_attention(fq, fk, fv, ff, qn, kn, fb),
    ], axis=-1)
    return y @ w_out


def moe(h, router_w, router_bias, w_gate, w_up, w_down, sw_gate, sw_up, sw_down):
    B, S, D = h.shape
    t = h.reshape(B * S, D)
    scores = jax.nn.sigmoid((t @ router_w).astype(jnp.float32))
    _, idx = lax.top_k(scores + router_bias, TOP_K)
    sel = jnp.take_along_axis(scores, idx, axis=-1)
    wts = sel / jnp.sum(sel, axis=-1, keepdims=True) * ROUTED_SCALE
    gates = jnp.sum(jax.nn.one_hot(idx, N_EXPERTS, dtype=wts.dtype) * wts[..., None], axis=1)
    hidden = jax.nn.silu(jnp.einsum('td,edf->tef', t, w_gate)) * jnp.einsum('td,edf->tef', t, w_up)
    routed = jnp.einsum('tef,efd->td', hidden * gates[:, :, None].astype(hidden.dtype), w_down)
    shared = (jax.nn.silu(t @ sw_gate) * (t @ sw_up)) @ sw_down
    return (routed + shared).reshape(B, S, D)


def setup_inputs(seed: int = 0) -> dict:
    key = jax.random.key(seed)
    ks = jax.random.split(key, 32)
    nrm = lambda k, shape, s: jax.random.normal(k, shape, jnp.float32) * s
    L, D, G, H, Dh, E, F = DEPTH, D_MODEL, GROUP_W, N_HEADS, HEAD_DIM, N_EXPERTS, D_EXPERT
    u = jax.random.uniform(ks[15], (L, G), jnp.float32, minval=0.9, maxval=0.999)
    a0 = u ** (1.0 / LRU_C)
    return {
        'x': nrm(ks[0], (BATCH, SEQ, D), 1.0),
        'c': nrm(ks[1], (BATCH, D), 1.0),
        'w_ada': nrm(ks[2], (D, N_MOD * D), 0.1 * D ** -0.5),
        'b_ada': nrm(ks[3], (N_MOD * D,), 0.01),
        'ada_table': nrm(ks[4], (L, N_MOD, D), 0.2),
        'norm1_g': 1.0 + nrm(ks[5], (L, D), 0.02),
        'norm2_g': 1.0 + nrm(ks[6], (L, D), 0.02),
        'w_in': nrm(ks[7], (L, D, N_IN), D ** -0.5),
        'w_out': nrm(ks[8], (L, D_MIX, D), D_MIX ** -0.5),
        'lru_conv_w': nrm(ks[9], (L, CONV_W, G), CONV_W ** -0.5),
        'lru_conv_b': nrm(ks[10], (L, G), 0.01),
        'lru_wa': nrm(ks[11], (L, H, Dh, Dh), Dh ** -0.5),
        'lru_ba': nrm(ks[12], (L, G), 0.01),
        'lru_wx': nrm(ks[13], (L, H, Dh, Dh), Dh ** -0.5),
        'lru_bx': nrm(ks[14], (L, G), 0.01),
        'lru_lambda': jnp.log(a0) - jnp.log1p(-a0),
        'sgu_norm_g': 1.0 + nrm(ks[16], (L, G), 0.02),
        'sgu_w': nrm(ks[17], (L, H, CHUNK, CHUNK), 0.5 * CHUNK ** -0.5),
        'sgu_b': 1.0 + nrm(ks[18], (L, H, CHUNK), 0.02),
        'fox_qn': 1.0 + nrm(ks[19], (L, Dh), 0.02),
        'fox_kn': 1.0 + nrm(ks[20], (L, Dh), 0.02),
        'fox_fb': 3.0 + nrm(ks[21], (L, H), 0.5),
        'router_w': nrm(ks[22], (L, D, E), D ** -0.5),
        'router_bias': nrm(ks[23], (L, E), 0.01),
        'exp_w_gate': nrm(ks[24], (L, E, D, F), D ** -0.5),
        'exp_w_up': nrm(ks[25], (L, E, D, F), D ** -0.5),
        'exp_w_down': nrm(ks[26], (L, E, F, D), F ** -0.5),
        'sh_w_gate': nrm(ks[27], (L, D, F), D ** -0.5),
        'sh_w_up': nrm(ks[28], (L, D, F), D ** -0.5),
        'sh_w_down': nrm(ks[29], (L, F, D), F ** -0.5),
    }


def reference(x, c, w_ada, b_ada, ada_table, norm1_g, norm2_g, w_in, w_out,
              lru_conv_w, lru_conv_b, lru_wa, lru_ba, lru_wx, lru_bx, lru_lambda,
              sgu_norm_g, sgu_w, sgu_b, fox_qn, fox_kn, fox_fb,
              router_w, router_bias, exp_w_gate, exp_w_up, exp_w_down,
              sh_w_gate, sh_w_up, sh_w_down):
    B = x.shape[0]
    mod_shared = (jax.nn.silu(c) @ w_ada + b_ada).reshape(B, N_MOD, D_MODEL)
    for l in range(DEPTH):
        mod = mod_shared + ada_table[l][None]
        shift1, scale1, gate1, shift2, scale2, gate2 = [mod[:, i][:, None, :] for i in range(N_MOD)]
        h = rms_norm(x, norm1_g[l]) * (1.0 + scale1) + shift1
        mix = hybrid_mixer(h, w_in[l], w_out[l], lru_conv_w[l], lru_conv_b[l], lru_wa[l], lru_ba[l],
                           lru_wx[l], lru_bx[l], lru_lambda[l], sgu_norm_g[l], sgu_w[l], sgu_b[l],
                           fox_qn[l], fox_kn[l], fox_fb[l])
        x = x + gate1 * mix
        h = rms_norm(x, norm2_g[l]) * (1.0 + scale2) + shift2
        x = x + gate2 * moe(h, router_w[l], router_bias[l], exp_w_gate[l], exp_w_up[l], exp_w_down[l],
                            sh_w_gate[l], sh_w_up[l], sh_w_down[l])
    return x
```

```python
import functools
import math

import numpy as np
import jax
import jax.numpy as jnp
from jax import lax
from jax.experimental import pallas as pl
from jax.experimental.pallas import tpu as pltpu

F32 = jnp.float32
BF16 = jnp.bfloat16

D_MODEL = 4096
GROUP_W = 1024
HEAD_DIM = 128
N_HEADS = 8
CHUNK = 128
CONV_W = 4
LRU_C = 8.0
ROPE_BASE = 10000.0
N_EXPERTS = 64
TOP_K = 8
D_EXPERT = 256
ROUTED_SCALE = 2.5
N_MOD = 6
EPS = 1e-6
N_IN_BLOCKS = 11
N_PROJ = N_IN_BLOCKS * GROUP_W
LANES = 128
SUBLANES = 8
VMEM_LIMIT = 56 * 1024 * 1024
NEG_BIG = -1e30


def _cparams(*sem):
    return pltpu.CompilerParams(dimension_semantics=sem, vmem_limit_bytes=VMEM_LIMIT)


def _silu(x):
    return x * jax.nn.sigmoid(x)


def _gelu_tanh(x):
    c = math.sqrt(2.0 / math.pi)
    return 0.5 * x * (1.0 + jnp.tanh(c * (x + 0.044715 * (x * x * x))))


def _mod_row(mod_ref, b, k):
    return mod_ref[pl.ds(b * N_MOD + k, 1), :]


def _ada_kernel(c_ref, w_ref, b_ref, tab_ref, o_ref):
    a = _silu(c_ref[...]).astype(BF16)
    acc = jnp.dot(a, w_ref[...].astype(BF16), preferred_element_type=F32) + b_ref[...]
    for l in range(o_ref.shape[0]):
        o_ref[l] = acc + tab_ref[l:l + 1, :]


def _ada_mod(c_pad, w_ada, b_ada, ada_table):
    bp, d = c_pad.shape
    n = w_ada.shape[1]
    depth = ada_table.shape[0]
    tn = 512
    return pl.pallas_call(
        _ada_kernel,
        grid=(n // tn,),
        in_specs=[
            pl.BlockSpec((bp, d), lambda j: (0, 0)),
            pl.BlockSpec((d, tn), lambda j: (0, j)),
            pl.BlockSpec((1, tn), lambda j: (0, j)),
            pl.BlockSpec((depth, tn), lambda j: (0, j)),
        ],
        out_specs=pl.BlockSpec((depth, bp, tn), lambda j: (0, 0, j)),
        out_shape=jax.ShapeDtypeStruct((depth, bp, n), F32),
        compiler_params=_cparams("arbitrary"),
        name="ada_mod",
    )(c_pad, w_ada, b_ada.reshape(1, n), ada_table.reshape(depth, n))


def _norm_mod_kernel(x_ref, g_ref, mod_ref, o_ref, *, shift_idx, scale_idx, tiles_per_batch):
    b = pl.program_id(0) // tiles_per_batch
    x = x_ref[...]
    ms = jnp.mean(x * x, axis=-1, keepdims=True)
    y = x * lax.rsqrt(ms + EPS) * g_ref[...]
    h = y * (1.0 + _mod_row(mod_ref, b, scale_idx)) + _mod_row(mod_ref, b, shift_idx)
    o_ref[...] = h.astype(o_ref.dtype)


def _norm_mod(x2, g, mod, seq, shift_idx, scale_idx):
    t, d = x2.shape
    tm = 256
    return pl.pallas_call(
        functools.partial(_norm_mod_kernel, shift_idx=shift_idx, scale_idx=scale_idx,
                          tiles_per_batch=seq // tm),
        grid=(t // tm,),
        in_specs=[
            pl.BlockSpec((tm, d), lambda i: (i, 0)),
            pl.BlockSpec((1, d), lambda i: (0, 0)),
            pl.BlockSpec(mod.shape, lambda i: (0, 0)),
        ],
        out_specs=pl.BlockSpec((tm, d), lambda i: (i, 0)),
        out_shape=jax.ShapeDtypeStruct((t, d), BF16),
        compiler_params=_cparams("arbitrary"),
        name="norm_mod",
    )(x2, g.reshape(1, d), mod)


def _mm_ws_kernel(a_ref, w_ref, o_ref, wb_ref):
    @pl.when(pl.program_id(1) == 0)
    def _():
        wb_ref[...] = w_ref[...].astype(BF16)

    o_ref[...] = jnp.dot(a_ref[...], wb_ref[...], preferred_element_type=F32).astype(o_ref.dtype)


def _mm_ws(a, w, n_out, tn, out_dtype, tm=1024):
    m, k = a.shape
    return pl.pallas_call(
        _mm_ws_kernel,
        grid=(n_out // tn, m // tm),
        in_specs=[
            pl.BlockSpec((tm, k), lambda j, i: (i, 0)),
            pl.BlockSpec((k, tn), lambda j, i: (0, j)),
        ],
        out_specs=pl.BlockSpec((tm, tn), lambda j, i: (i, j)),
        out_shape=jax.ShapeDtypeStruct((m, n_out), out_dtype),
        scratch_shapes=[pltpu.VMEM((k, tn), BF16)],
        compiler_params=_cparams("arbitrary", "arbitrary"),
        name="mm_ws",
    )(a, w)


def _retention_consts(seq):
    half = HEAD_DIM // 2
    inv = ROPE_BASE ** (-np.arange(half, dtype=np.float64) / half)
    ang = np.arange(seq, dtype=np.float64)[:, None] * inv[None, :]
    cos = np.cos(ang)
    sin = np.sin(ang)
    cos_full = np.tile(np.concatenate([cos, cos], axis=1), (1, N_HEADS))
    sin_signed = np.tile(np.concatenate([-sin, sin], axis=1), (1, N_HEADS))
    log_gamma = np.log1p(-(2.0 ** (-5.0 - np.arange(N_HEADS, dtype=np.float64))))
    idx = np.arange(CHUNK, dtype=np.float64)
    rel = idx[:, None] - idx[None, :]
    decay = np.where(rel >= 0, np.exp(np.where(rel >= 0, rel, 0.0)[None] * log_gamma[:, None, None]), 0.0)
    k_decay = np.exp((CHUNK - 1 - idx)[None, :] * log_gamma[:, None])
    q_decay = np.exp((idx + 1)[None, :] * log_gamma[:, None])
    kdec_full = np.repeat(k_decay.T, HEAD_DIM, axis=1)
    qdec_full = np.repeat(q_decay.T, HEAD_DIM, axis=1)
    chunk_decay = [float(v) for v in np.exp(CHUNK * log_gamma)]
    f = lambda v: jnp.asarray(v, dtype=F32)
    return f(cos_full), f(sin_signed), f(decay), f(kdec_full), f(qdec_full), chunk_decay


def _rotary_block(t, cos, sin_signed):
    parts = []
    for h in range(N_HEADS):
        sl = slice(h * HEAD_DIM, (h + 1) * HEAD_DIM)
        parts.append(pltpu.roll(t[:, sl], HEAD_DIM // 2, axis=1))
    rolled = jnp.concatenate(parts, axis=1)
    return t * cos + rolled * sin_signed


def _retention_kernel(q_ref, k_ref, v_ref, g_ref, cos_ref, sin_ref, decay_ref, kdec_ref, qdec_ref,
                      o_ref, state_ref, *, chunk_decay):
    @pl.when(pl.program_id(1) == 0)
    def _():
        state_ref[...] = jnp.zeros_like(state_ref)

    cos = cos_ref[...]
    sin = sin_ref[...]
    q = _rotary_block(q_ref[...].astype(F32), cos, sin)
    k = _rotary_block(k_ref[...].astype(F32), cos, sin) * (HEAD_DIM ** -0.5)
    qd = (q * qdec_ref[...]).astype(BF16)
    kd = (k * kdec_ref[...]).astype(BF16)
    qb = q.astype(BF16)
    kb = k.astype(BF16)
    v = v_ref[...]
    g = g_ref[...].astype(F32)
    for h in range(N_HEADS):
        sl = slice(h * HEAD_DIM, (h + 1) * HEAD_DIM)
        s = lax.dot_general(qb[:, sl], kb[:, sl], (((1,), (1,)), ((), ())),
                            preferred_element_type=F32) * decay_ref[h]
        intra = jnp.dot(s.astype(BF16), v[:, sl], preferred_element_type=F32)
        st = state_ref[h]
        inter = jnp.dot(qd[:, sl], st.astype(BF16), preferred_element_type=F32)
        kv = lax.dot_general(kd[:, sl], v[:, sl], (((0,), (0,)), ((), ())),
                             preferred_element_type=F32)
        state_ref[h] = chunk_decay[h] * st + kv
        y = intra + inter
        y = y * lax.rsqrt(jnp.mean(y * y, axis=-1, keepdims=True) + EPS)
        o_ref[:, sl] = (_silu(g[:, sl]) * y).astype(o_ref.dtype)


def _retention(proj, batch, seq):
    t = proj.shape[0]
    n = seq // CHUNK
    cos, sin, decay, kdec, qdec, chunk_decay = _retention_consts(seq)
    col = lambda c: pl.BlockSpec((CHUNK, GROUP_W), lambda b, i, c=c: (b * n + i, c))
    full2 = lambda arr: pl.BlockSpec(arr.shape, lambda b, i: (0,) * arr.ndim)
    return pl.pallas_call(
        functools.partial(_retention_kernel, chunk_decay=chunk_decay),
        grid=(batch, n),
        in_specs=[col(0), col(1), col(2), col(3),
                  pl.BlockSpec((CHUNK, GROUP_W), lambda b, i: (i, 0)),
                  pl.BlockSpec((CHUNK, GROUP_W), lambda b, i: (i, 0)),
                  full2(decay), full2(kdec), full2(qdec)],
        out_specs=pl.BlockSpec((CHUNK, GROUP_W), lambda b, i: (b * n + i, 0)),
        out_shape=jax.ShapeDtypeStruct((t, GROUP_W), BF16),
        scratch_shapes=[pltpu.VMEM((N_HEADS, HEAD_DIM, HEAD_DIM), F32)],
        compiler_params=_cparams("arbitrary", "arbitrary"),
        name="retention",
    )(proj, proj, proj, proj, cos, sin, decay, kdec, qdec)


def _scan_rows(a, b):
    rows = a.shape[0]
    row = lax.broadcasted_iota(jnp.int32, (rows, 1), 0)
    d = 1
    while d < rows:
        ar = pltpu.roll(a, d, axis=0)
        br = pltpu.roll(b, d, axis=0)
        m = row >= d
        b = jnp.where(m, a * br + b, b)
        a = jnp.where(m, a * ar, a)
        d *= 2
    return a, b


def _lru_kernel(lg_ref, lx_ref, cw_ref, cb_ref, wa_ref, ba_ref, wx_ref, bx_ref, lam_ref,
                o_ref, prev_ref, h_ref):
    @pl.when(pl.program_id(1) == 0)
    def _():
        prev_ref[...] = jnp.zeros_like(prev_ref)
        h_ref[...] = jnp.zeros_like(h_ref)

    x = lx_ref[...].astype(F32)
    ts = x.shape[0]
    prev = prev_ref[...]
    row8 = lax.broadcasted_iota(jnp.int32, (SUBLANES, 1), 0)
    xc = x * cw_ref[CONV_W - 1:CONV_W, :] + cb_ref[...]
    for sh in range(1, CONV_W):
        xr = pltpu.roll(x, sh, axis=0)
        pr = pltpu.roll(prev, sh, axis=0)
        top = jnp.where(row8 < sh, pr, xr[:SUBLANES])
        xs = jnp.concatenate([top, xr[SUBLANES:]], axis=0)
        xc = xc + xs * cw_ref[CONV_W - 1 - sh:CONV_W - sh, :]
    prev_ref[...] = x[ts - SUBLANES:]

    xcb = xc.astype(BF16)
    rs, is_ = [], []
    for h in range(N_HEADS):
        sl = slice(h * HEAD_DIM, (h + 1) * HEAD_DIM)
        rs.append(jnp.dot(xcb[:, sl], wa_ref[h].astype(BF16), preferred_element_type=F32))
        is_.append(jnp.dot(xcb[:, sl], wx_ref[h].astype(BF16), preferred_element_type=F32))
    r = jax.nn.sigmoid(jnp.concatenate(rs, axis=1) + ba_ref[...])
    i = jax.nn.sigmoid(jnp.concatenate(is_, axis=1) + bx_ref[...])
    lam = lam_ref[...]
    softplus_neg_lam = jnp.maximum(-lam, 0.0) + jnp.log1p(jnp.exp(-jnp.abs(lam)))
    log_a = -LRU_C * r * softplus_neg_lam
    a = jnp.exp(log_a)
    th = jnp.tanh(log_a)
    b = jnp.sqrt(-2.0 * th / (1.0 - th)) * (i * xc)
    a_cum, hs = _scan_rows(a, b)
    hcur = a_cum * h_ref[SUBLANES - 1:SUBLANES, :] + hs
    h_ref[...] = hcur[ts - SUBLANES:]
    o_ref[...] = (hcur * _gelu_tanh(lg_ref[...].astype(F32))).astype(o_ref.dtype)


def _lru(proj, batch, seq, conv_w, conv_b, wa, ba, wx, bx, lam):
    t = proj.shape[0]
    ts = 256
    n = seq // ts
    row = lambda v: v.reshape(1, GROUP_W)
    vec = pl.BlockSpec((1, GROUP_W), lambda b, i: (0, 0))
    mat = pl.BlockSpec((N_HEADS, HEAD_DIM, HEAD_DIM), lambda b, i: (0, 0, 0))
    return pl.pallas_call(
        _lru_kernel,
        grid=(batch, n),
        in_specs=[pl.BlockSpec((ts, GROUP_W), lambda b, i: (b * n + i, 4)),
                  pl.BlockSpec((ts, GROUP_W), lambda b, i: (b * n + i, 5)),
                  pl.BlockSpec((CONV_W, GROUP_W), lambda b, i: (0, 0)),
                  vec, mat, vec, mat, vec, vec],
        out_specs=pl.BlockSpec((ts, GROUP_W), lambda b, i: (b * n + i, 0)),
        out_shape=jax.ShapeDtypeStruct((t, GROUP_W), BF16),
        scratch_shapes=[pltpu.VMEM((SUBLANES, GROUP_W), F32), pltpu.VMEM((SUBLANES, GROUP_W), F32)],
        compiler_params=_cparams("arbitrary", "arbitrary"),
        name="rg_lru",
    )(proj, proj, conv_w, row(conv_b), wa, row(ba), wx, row(bx), row(lam))


def _sgu_kernel(u_ref, v_ref, ng_ref, w_ref, bias_ref, o_ref):
    v = _gelu_tanh(v_ref[...].astype(F32))
    v = v * lax.rsqrt(jnp.mean(v * v, axis=-1, keepdims=True) + EPS) * ng_ref[...]
    vb = v.astype(BF16)
    u = _gelu_tanh(u_ref[...].astype(F32))
    r = lax.broadcasted_iota(jnp.int32, (CHUNK, CHUNK), 0)
    c = lax.broadcasted_iota(jnp.int32, (CHUNK, CHUNK), 1)
    tril = r >= c
    bias = bias_ref[...]
    for g in range(N_HEADS):
        sl = slice(g * HEAD_DIM, (g + 1) * HEAD_DIM)
        w = jnp.where(tril, w_ref[g], 0.0).astype(BF16)
        for ch in range(u.shape[0] // CHUNK):
            rs = slice(ch * CHUNK, (ch + 1) * CHUNK)
            mixed = jnp.dot(w, vb[rs, sl], preferred_element_type=F32) + bias[:, sl]
            o_ref[rs, sl] = (u[rs, sl] * mixed).astype(o_ref.dtype)


def _sgu(proj, norm_g, w_s, b_s):
    t = proj.shape[0]
    ts = 512
    bias_full = jnp.repeat(b_s.T, HEAD_DIM, axis=1)
    return pl.pallas_call(
        _sgu_kernel,
        grid=(t // ts,),
        in_specs=[pl.BlockSpec((ts, GROUP_W), lambda i: (i, 6)),
                  pl.BlockSpec((ts, GROUP_W), lambda i: (i, 7)),
                  pl.BlockSpec((1, GROUP_W), lambda i: (0, 0)),
                  pl.BlockSpec((N_HEADS, CHUNK, CHUNK), lambda i: (0, 0, 0)),
                  pl.BlockSpec((CHUNK, GROUP_W), lambda i: (0, 0))],
        out_specs=pl.BlockSpec((ts, GROUP_W), lambda i: (i, 0)),
        out_shape=jax.ShapeDtypeStruct((t, GROUP_W), BF16),
        compiler_params=_cparams("arbitrary"),
        name="sgu",
    )(proj, proj, norm_g.reshape(1, GROUP_W), w_s, bias_full)


def _fox_prep_kernel(f_ref, fb_ref, cum_ref, cumt_ref):
    z = f_ref[...] + fb_ref[...]
    logf = jnp.minimum(z, 0.0) - jnp.log1p(jnp.exp(-jnp.abs(z)))
    rows = z.shape[0]
    row = lax.broadcasted_iota(jnp.int32, (rows, 1), 0)
    d = 1
    while d < rows:
        logf = logf + jnp.where(row >= d, pltpu.roll(logf, d, axis=0), 0.0)
        d *= 2
    cum_ref[...] = logf
    cumt_ref[...] = logf.T[:SUBLANES, :]


def _fox_prep(f_logit, fb_pad, batch, seq):
    return pl.pallas_call(
        _fox_prep_kernel,
        grid=(batch,),
        in_specs=[pl.BlockSpec((seq, LANES), lambda b: (b, 0)),
                  pl.BlockSpec((1, LANES), lambda b: (0, 0))],
        out_specs=[pl.BlockSpec((seq, LANES), lambda b: (b, 0)),
                   pl.BlockSpec((SUBLANES, seq), lambda b: (b, 0))],
        out_shape=[jax.ShapeDtypeStruct((batch * seq, LANES), F32),
                   jax.ShapeDtypeStruct((batch * SUBLANES, seq), F32)],
        compiler_params=_cparams("arbitrary"),
        name="fox_prep",
    )(f_logit, fb_pad)


def _qk_norm_kernel(q_ref, k_ref, qn_ref, kn_ref, qo_ref, ko_ref):
    for src, gain, dst, mult in ((q_ref, qn_ref, qo_ref, HEAD_DIM ** -0.5), (k_ref, kn_ref, ko_ref, 1.0)):
        x = src[...].astype(F32)
        for h in range(N_HEADS):
            sl = slice(h * HEAD_DIM, (h + 1) * HEAD_DIM)
            xh = x[:, sl]
            y = xh * lax.rsqrt(jnp.mean(xh * xh, axis=-1, keepdims=True) + EPS) * gain[...]
            dst[:, sl] = (y * mult).astype(dst.dtype)


def _qk_norm(proj, qn, kn):
    t = proj.shape[0]
    ts = 512
    out = jax.ShapeDtypeStruct((t, GROUP_W), BF16)
    return pl.pallas_call(
        _qk_norm_kernel,
        grid=(t // ts,),
        in_specs=[pl.BlockSpec((ts, GROUP_W), lambda i: (i, 8)),
                  pl.BlockSpec((ts, GROUP_W), lambda i: (i, 9)),
                  pl.BlockSpec((1, HEAD_DIM), lambda i: (0, 0)),
                  pl.BlockSpec((1, HEAD_DIM), lambda i: (0, 0))],
        out_specs=[pl.BlockSpec((ts, GROUP_W), lambda i: (i, 0))] * 2,
        out_shape=[out, out],
        compiler_params=_cparams("arbitrary"),
        name="qk_norm",
    )(proj, proj, qn.reshape(1, HEAD_DIM), kn.reshape(1, HEAD_DIM))


def _fox_kernel(q_ref, k_ref, v_ref, cum_ref, cumt_ref, o_ref, *, tq):
    h = pl.program_id(1)
    i = pl.program_id(2)
    q = q_ref[...]
    lane = lax.broadcasted_iota(jnp.int32, (tq, LANES), 1)
    cq = jnp.sum(jnp.where(lane == h, cum_ref[...], 0.0), axis=-1, keepdims=True)
    rowi = lax.broadcasted_iota(jnp.int32, (tq, tq), 0)
    coli = lax.broadcasted_iota(jnp.int32, (tq, tq), 1)

    def body(j, carry):
        m, l, acc = carry
        start = pl.multiple_of(j * tq, tq)
        kb = k_ref[pl.ds(start, tq), :]
        vb = v_ref[pl.ds(start, tq), :]
        ck = cumt_ref[pl.ds(h, 1), pl.ds(start, tq)]
        s = lax.dot_general(q, kb, (((1,), (1,)), ((), ())), preferred_element_type=F32)
        s = s + cq - ck
        s = jnp.where(coli + j * tq <= rowi + i * tq, s, NEG_BIG)
        m_new = jnp.maximum(m, jnp.max(s, axis=-1, keepdims=True))
        alpha = jnp.exp(m - m_new)
        p = jnp.exp(s - m_new)
        l = alpha * l + jnp.sum(p, axis=-1, keepdims=True)
        acc = alpha * acc + jnp.dot(p.astype(BF16), vb, preferred_element_type=F32)
        return m_new, l, acc

    m0 = jnp.full((tq, 1), NEG_BIG, F32)
    l0 = jnp.zeros((tq, 1), F32)
    a0 = jnp.zeros((tq, HEAD_DIM), F32)
    _, l, acc = lax.fori_loop(0, i + 1, body, (m0, l0, a0))
    o_ref[...] = (acc / l).astype(o_ref.dtype)


def _fox(qn, kn, proj, cum, cumt, batch, seq):
    t = qn.shape[0]
    tq = 256
    nq = seq // tq
    return pl.pallas_call(
        functools.partial(_fox_kernel, tq=tq),
        grid=(batch, N_HEADS, nq),
        in_specs=[pl.BlockSpec((tq, HEAD_DIM), lambda b, h, i: (b * nq + i, h)),
                  pl.BlockSpec((seq, HEAD_DIM), lambda b, h, i: (b, h)),
                  pl.BlockSpec((seq, HEAD_DIM), lambda b, h, i: (b, 10 * N_HEADS + h)),
                  pl.BlockSpec((tq, LANES), lambda b, h, i: (b * nq + i, 0)),
                  pl.BlockSpec((SUBLANES, seq), lambda b, h, i: (b, 0))],
        out_specs=pl.BlockSpec((tq, HEAD_DIM), lambda b, h, i: (b * nq + i, h)),
        out_shape=jax.ShapeDtypeStruct((t, GROUP_W), BF16),
        compiler_params=_cparams("arbitrary", "arbitrary", "arbitrary"),
        name="fox_attn",
    )(qn, kn, proj, cum, cumt)


def _out_proj_kernel(y0_ref, y1_ref, y2_ref, y3_ref, w_ref, x_ref, mod_ref, o_ref, wb_ref,
                     *, gate_idx, tiles_per_batch):
    @pl.when(pl.program_id(1) == 0)
    def _():
        wb_ref[...] = w_ref[...].astype(BF16)

    b = pl.program_id(1) // tiles_per_batch
    j = pl.program_id(0)
    tn = o_ref.shape[1]
    acc = jnp.zeros(o_ref.shape, F32)
    for g, y_ref in enumerate((y0_ref, y1_ref, y2_ref, y3_ref)):
        acc = acc + jnp.dot(y_ref[...], wb_ref[g * GROUP_W:(g + 1) * GROUP_W, :],
                            preferred_element_type=F32)
    gate = mod_ref[pl.ds(b * N_MOD + gate_idx, 1), pl.ds(pl.multiple_of(j * tn, tn), tn)]
    o_ref[...] = x_ref[...] + gate * acc


def _out_proj(ys, w_out, x2, mod, seq, gate_idx):
    t, d = x2.shape
    tm, tn = 1024, 512
    yspec = pl.BlockSpec((tm, GROUP_W), lambda j, i: (i, 0))
    return pl.pallas_call(
        functools.partial(_out_proj_kernel, gate_idx=gate_idx, tiles_per_batch=seq // tm),
        grid=(d // tn, t // tm),
        in_specs=[yspec, yspec, yspec, yspec,
                  pl.BlockSpec((w_out.shape[0], tn), lambda j, i: (0, j)),
                  pl.BlockSpec((tm, tn), lambda j, i: (i, j)),
                  pl.BlockSpec(mod.shape, lambda j, i: (0, 0))],
        out_specs=pl.BlockSpec((tm, tn), lambda j, i: (i, j)),
        out_shape=jax.ShapeDtypeStruct((t, d), F32),
        scratch_shapes=[pltpu.VMEM((w_out.shape[0], tn), BF16)],
        compiler_params=_cparams("arbitrary", "arbitrary"),
        name="out_proj",
    )(*ys, w_out, x2, mod)


def _split_bf16(x):
    hi = x.astype(BF16)
    lo = (x - hi.astype(F32)).astype(BF16)
    return hi, lo


def _router_kernel(x_ref, g_ref, mod_ref, rw_ref, rb_ref, h_ref, gates_ref,
                   *, shift_idx, scale_idx, tiles_per_batch):
    b = pl.program_id(0) // tiles_per_batch
    x = x_ref[...]
    ms = jnp.mean(x * x, axis=-1, keepdims=True)
    y = x * lax.rsqrt(ms + EPS) * g_ref[...]
    h = y * (1.0 + _mod_row(mod_ref, b, scale_idx)) + _mod_row(mod_ref, b, shift_idx)
    h_ref[...] = h.astype(h_ref.dtype)

    h_hi, h_lo = _split_bf16(h)
    w_hi, w_lo = _split_bf16(rw_ref[...])
    logits = (jnp.dot(h_hi, w_hi, preferred_element_type=F32)
              + jnp.dot(h_lo, w_hi, preferred_element_type=F32)
              + jnp.dot(h_hi, w_lo, preferred_element_type=F32))
    scores = jax.nn.sigmoid(logits)
    biased = scores + rb_ref[...]
    lane = lax.broadcasted_iota(jnp.int32, scores.shape, 1)
    sel = jnp.zeros(scores.shape, F32)
    for _ in range(TOP_K):
        mx = jnp.max(biased, axis=-1, keepdims=True)
        first = jnp.min(jnp.where(biased == mx, lane, N_EXPERTS), axis=-1, keepdims=True)
        hit = lane == first
        sel = jnp.where(hit, scores, sel)
        biased = jnp.where(hit, -jnp.inf, biased)
    gates_ref[...] = sel / jnp.sum(sel, axis=-1, keepdims=True) * ROUTED_SCALE


def _router(x2, g, mod, router_w, router_bias, seq, shift_idx, scale_idx):
    t, d = x2.shape
    tm = 256
    return pl.pallas_call(
        functools.partial(_router_kernel, shift_idx=shift_idx, scale_idx=scale_idx,
                          tiles_per_batch=seq // tm),
        grid=(t // tm,),
        in_specs=[pl.BlockSpec((tm, d), lambda i: (i, 0)),
                  pl.BlockSpec((1, d), lambda i: (0, 0)),
                  pl.BlockSpec(mod.shape, lambda i: (0, 0)),
                  pl.BlockSpec((d, N_EXPERTS), lambda i: (0, 0)),
                  pl.BlockSpec((1, N_EXPERTS), lambda i: (0, 0))],
        out_specs=[pl.BlockSpec((tm, d), lambda i: (i, 0)),
                   pl.BlockSpec((tm, N_EXPERTS), lambda i: (i, 0))],
        out_shape=[jax.ShapeDtypeStruct((t, d), BF16),
                   jax.ShapeDtypeStruct((t, N_EXPERTS), F32)],
        compiler_params=_cparams("arbitrary"),
        name="norm_router",
    )(x2, g.reshape(1, d), mod, router_w, router_bias.reshape(1, N_EXPERTS))


def _expert_up_kernel(h_ref, wg_ref, wu_ref, gates_ref, o_ref, wgb_ref, wub_ref, *, routed):
    @pl.when(pl.program_id(1) == 0)
    def _():
        wgb_ref[...] = wg_ref[...].astype(BF16)
        wub_ref[...] = wu_ref[...].astype(BF16)

    hb = h_ref[...]
    gt = jnp.dot(hb, wgb_ref[...], preferred_element_type=F32)
    up = jnp.dot(hb, wub_ref[...], preferred_element_type=F32)
    hid = _silu(gt) * up
    if routed:
        e = pl.program_id(0)
        gates = gates_ref[...]
        lane = lax.broadcasted_iota(jnp.int32, gates.shape, 1)
        hid = hid * jnp.sum(jnp.where(lane == e, gates, 0.0), axis=-1, keepdims=True)
    o_ref[...] = hid.astype(o_ref.dtype)


def _expert_up(h, w_gate, w_up, gates, routed):
    t, d = h.shape
    e, _, f = w_gate.shape
    tm = 1024
    return pl.pallas_call(
        functools.partial(_expert_up_kernel, routed=routed),
        grid=(e, t // tm),
        in_specs=[pl.BlockSpec((tm, d), lambda j, i: (i, 0)),
                  pl.BlockSpec((None, d, f), lambda j, i: (j, 0, 0)),
                  pl.BlockSpec((None, d, f), lambda j, i: (j, 0, 0)),
                  pl.BlockSpec((tm, gates.shape[1]), lambda j, i: (i, 0))],
        out_specs=pl.BlockSpec((tm, f), lambda j, i: (i, j)),
        out_shape=jax.ShapeDtypeStruct((t, e * f), BF16),
        scratch_shapes=[pltpu.VMEM((d, f), BF16), pltpu.VMEM((d, f), BF16)],
        compiler_params=_cparams("arbitrary", "arbitrary"),
        name="expert_up",
    )(h, w_gate, w_up, gates)


def _expert_down_kernel(a_ref, w_ref, sa_ref, sw_ref, x_ref, mod_ref, o_ref, acc_ref,
                        *, gate_idx, tiles_per_batch):
    kk = pl.program_id(2)

    @pl.when(kk == 0)
    def _():
        acc_ref[...] = jnp.dot(sa_ref[...], sw_ref[...].astype(BF16), preferred_element_type=F32)

    acc_ref[...] += jnp.dot(a_ref[...], w_ref[...].astype(BF16), preferred_element_type=F32)

    @pl.when(kk == pl.num_programs(2) - 1)
    def _():
        b = pl.program_id(0) // tiles_per_batch
        j = pl.program_id(1)
        tn = o_ref.shape[1]
        gate = mod_ref[pl.ds(b * N_MOD + gate_idx, 1), pl.ds(pl.multiple_of(j * tn, tn), tn)]
        o_ref[...] = x_ref[...] + gate * acc_ref[...]


def _expert_down(hidden, w_down2, sh_hidden, sw_down, x2, mod, seq, gate_idx):
    t, kdim = hidden.shape
    d = x2.shape[1]
    f = sh_hidden.shape[1]
    tm, tn, tk = 1024, min(1024, d), 512
    return pl.pallas_call(
        functools.partial(_expert_down_kernel, gate_idx=gate_idx, tiles_per_batch=seq // tm),
        grid=(t // tm, d // tn, kdim // tk),
        in_specs=[pl.BlockSpec((tm, tk), lambda i, j, k: (i, k)),
                  pl.BlockSpec((tk, tn), lambda i, j, k: (k, j)),
                  pl.BlockSpec((tm, f), lambda i, j, k: (i, 0)),
                  pl.BlockSpec((f, tn), lambda i, j, k: (0, j)),
                  pl.BlockSpec((tm, tn), lambda i, j, k: (i, j)),
                  pl.BlockSpec(mod.shape, lambda i, j, k: (0, 0))],
        out_specs=pl.BlockSpec((tm, tn), lambda i, j, k: (i, j)),
        out_shape=jax.ShapeDtypeStruct((t, d), F32),
        scratch_shapes=[pltpu.VMEM((tm, tn), F32)],
        compiler_params=_cparams("arbitrary", "arbitrary", "arbitrary"),
        name="expert_down",
    )(hidden, w_down2, sh_hidden, sw_down, x2, mod)


def kernel(x, c, w_ada, b_ada, ada_table, norm1_g, norm2_g, w_in, w_out, lru_conv_w, lru_conv_b, lru_wa, lru_ba, lru_wx, lru_bx, lru_lambda, sgu_norm_g, sgu_w, sgu_b, fox_qn, fox_kn, fox_fb, router_w, router_bias, exp_w_gate, exp_w_up, exp_w_down, sh_w_gate, sh_w_up, sh_w_down):
    batch, seq, d = x.shape
    depth = ada_table.shape[0]
    t = batch * seq
    bp = SUBLANES * ((batch + SUBLANES - 1) // SUBLANES)
    c_pad = jnp.pad(c, ((0, bp - batch), (0, 0)))
    mod_all = _ada_mod(c_pad, w_ada, b_ada, ada_table)
    x2 = x.reshape(t, d)
    for l in range(depth):
        mod = mod_all[l].reshape(bp * N_MOD, d)
        h = _norm_mod(x2, norm1_g[l], mod, seq, shift_idx=0, scale_idx=1)
        proj = _mm_ws(h, w_in[l], N_PROJ, 512, BF16)
        w_f = jnp.pad(w_in[l][:, N_PROJ:], ((0, 0), (0, LANES - N_HEADS)))
        f_logit = _mm_ws(h, w_f, LANES, LANES, F32)
        fb_pad = jnp.pad(fox_fb[l], (0, LANES - N_HEADS)).reshape(1, LANES)
        y_ret = _retention(proj, batch, seq)
        y_lru = _lru(proj, batch, seq, lru_conv_w[l], lru_conv_b[l], lru_wa[l], lru_ba[l],
                     lru_wx[l], lru_bx[l], lru_lambda[l])
        y_sgu = _sgu(proj, sgu_norm_g[l], sgu_w[l], sgu_b[l])
        cum, cumt = _fox_prep(f_logit, fb_pad, batch, seq)
        qn, kn = _qk_norm(proj, fox_qn[l], fox_kn[l])
        y_fox = _fox(qn, kn, proj, cum, cumt, batch, seq)
        x2 = _out_proj((y_ret, y_lru, y_sgu, y_fox), w_out[l], x2, mod, seq, gate_idx=2)
        h2, gates = _router(x2, norm2_g[l], mod, router_w[l], router_bias[l], seq,
                            shift_idx=3, scale_idx=4)
        hidden = _expert_up(h2, exp_w_gate[l], exp_w_up[l], gates, routed=True)
        sh_hidden = _expert_up(h2, sh_w_gate[l][None], sh_w_up[l][None], gates, routed=False)
        w_down2 = exp_w_down[l].reshape(N_EXPERTS * D_EXPERT, d)
        x2 = _expert_down(hidden, w_down2, sh_hidden, sh_w_down[l], x2, mod, seq, gate_idx=5)
    return x2.reshape(batch, seq, d)
```

```python
import functools
import math

import numpy as np
import jax
import jax.numpy as jnp
from jax import lax
from jax.experimental import pallas as pl
from jax.experimental.pallas import tpu as pltpu

F32 = jnp.float32
BF16 = jnp.bfloat16
U32 = jnp.uint32

D_MODEL = 4096
GROUP_W = 1024
HEAD_DIM = 128
N_HEADS = 8
CHUNK = 128
CONV_W = 4
LRU_C = 8.0
ROPE_BASE = 10000.0
N_EXPERTS = 64
TOP_K = 8
D_EXPERT = 256
ROUTED_SCALE = 2.5
N_MOD = 6
EPS = 1e-6
N_IN_BLOCKS = 11
N_PROJ = N_IN_BLOCKS * GROUP_W
LANES = 128
SUBLANES = 8
VMEM_LIMIT = 56 * 1024 * 1024
NEG_BIG = -1e30
MOE_TILE = 256


def _cparams(*sem):
    return pltpu.CompilerParams(dimension_semantics=sem, vmem_limit_bytes=VMEM_LIMIT)


def _silu(x):
    return x * jax.nn.sigmoid(x)


def _gelu_tanh(x):
    c = math.sqrt(2.0 / math.pi)
    return 0.5 * x * (1.0 + jnp.tanh(c * (x + 0.044715 * (x * x * x))))


def _mod_row(mod_ref, b, k):
    return mod_ref[pl.ds(b * N_MOD + k, 1), :]


def _ada_kernel(c_ref, w_ref, b_ref, tab_ref, o_ref):
    a = _silu(c_ref[...]).astype(BF16)
    acc = jnp.dot(a, w_ref[...].astype(BF16), preferred_element_type=F32) + b_ref[...]
    for l in range(o_ref.shape[0]):
        o_ref[l] = acc + tab_ref[l:l + 1, :]


def _ada_mod(c_pad, w_ada, b_ada, ada_table):
    bp, d = c_pad.shape
    n = w_ada.shape[1]
    depth = ada_table.shape[0]
    tn = 512
    return pl.pallas_call(
        _ada_kernel,
        grid=(n // tn,),
        in_specs=[
            pl.BlockSpec((bp, d), lambda j: (0, 0)),
            pl.BlockSpec((d, tn), lambda j: (0, j)),
            pl.BlockSpec((1, tn), lambda j: (0, j)),
            pl.BlockSpec((depth, tn), lambda j: (0, j)),
        ],
        out_specs=pl.BlockSpec((depth, bp, tn), lambda j: (0, 0, j)),
        out_shape=jax.ShapeDtypeStruct((depth, bp, n), F32),
        compiler_params=_cparams("arbitrary"),
        name="ada_mod",
    )(c_pad, w_ada, b_ada.reshape(1, n), ada_table.reshape(depth, n))


def _norm_mod_kernel(x_ref, g_ref, mod_ref, o_ref, *, shift_idx, scale_idx, tiles_per_batch):
    b = pl.program_id(0) // tiles_per_batch
    x = x_ref[...]
    ms = jnp.mean(x * x, axis=-1, keepdims=True)
    y = x * lax.rsqrt(ms + EPS) * g_ref[...]
    h = y * (1.0 + _mod_row(mod_ref, b, scale_idx)) + _mod_row(mod_ref, b, shift_idx)
    o_ref[...] = h.astype(o_ref.dtype)


def _norm_mod(x2, g, mod, seq, shift_idx, scale_idx):
    t, d = x2.shape
    tm = 256
    return pl.pallas_call(
        functools.partial(_norm_mod_kernel, shift_idx=shift_idx, scale_idx=scale_idx,
                          tiles_per_batch=seq // tm),
        grid=(t // tm,),
        in_specs=[
            pl.BlockSpec((tm, d), lambda i: (i, 0)),
            pl.BlockSpec((1, d), lambda i: (0, 0)),
            pl.BlockSpec(mod.shape, lambda i: (0, 0)),
        ],
        out_specs=pl.BlockSpec((tm, d), lambda i: (i, 0)),
        out_shape=jax.ShapeDtypeStruct((t, d), BF16),
        compiler_params=_cparams("arbitrary"),
        name="norm_mod",
    )(x2, g.reshape(1, d), mod)


def _mm_ws_kernel(a_ref, w_ref, o_ref, wb_ref):
    @pl.when(pl.program_id(1) == 0)
    def _():
        wb_ref[...] = w_ref[...].astype(BF16)

    o_ref[...] = jnp.dot(a_ref[...], wb_ref[...], preferred_element_type=F32).astype(o_ref.dtype)


def _mm_ws(a, w, layer, n_out, tn, out_dtype, tm=1024):
    m, k = a.shape
    return pl.pallas_call(
        _mm_ws_kernel,
        grid=(n_out // tn, m // tm),
        in_specs=[
            pl.BlockSpec((tm, k), lambda j, i: (i, 0)),
            pl.BlockSpec((None, k, tn), lambda j, i: (layer, 0, j)),
        ],
        out_specs=pl.BlockSpec((tm, tn), lambda j, i: (i, j)),
        out_shape=jax.ShapeDtypeStruct((m, n_out), out_dtype),
        scratch_shapes=[pltpu.VMEM((k, tn), BF16)],
        compiler_params=_cparams("arbitrary", "arbitrary"),
        name="mm_ws",
    )(a, w)


def _retention_consts(seq):
    half = HEAD_DIM // 2
    inv = ROPE_BASE ** (-np.arange(half, dtype=np.float64) / half)
    ang = np.arange(seq, dtype=np.float64)[:, None] * inv[None, :]
    cos = np.cos(ang)
    sin = np.sin(ang)
    cos_full = np.tile(np.concatenate([cos, cos], axis=1), (1, N_HEADS))
    sin_signed = np.tile(np.concatenate([-sin, sin], axis=1), (1, N_HEADS))
    log_gamma = np.log1p(-(2.0 ** (-5.0 - np.arange(N_HEADS, dtype=np.float64))))
    idx = np.arange(CHUNK, dtype=np.float64)
    rel = idx[:, None] - idx[None, :]
    decay = np.where(rel >= 0, np.exp(np.where(rel >= 0, rel, 0.0)[None] * log_gamma[:, None, None]), 0.0)
    k_decay = np.exp((CHUNK - 1 - idx)[None, :] * log_gamma[:, None])
    q_decay = np.exp((idx + 1)[None, :] * log_gamma[:, None])
    kdec_full = np.repeat(k_decay.T, HEAD_DIM, axis=1)
    qdec_full = np.repeat(q_decay.T, HEAD_DIM, axis=1)
    chunk_decay = [float(v) for v in np.exp(CHUNK * log_gamma)]
    f = lambda v: jnp.asarray(v, dtype=F32)
    return f(cos_full), f(sin_signed), f(decay), f(kdec_full), f(qdec_full), chunk_decay


def _rotary_block(t, cos, sin_signed):
    parts = []
    for h in range(N_HEADS):
        sl = slice(h * HEAD_DIM, (h + 1) * HEAD_DIM)
        parts.append(pltpu.roll(t[:, sl], HEAD_DIM // 2, axis=1))
    rolled = jnp.concatenate(parts, axis=1)
    return t * cos + rolled * sin_signed


def _retention_kernel(q_ref, k_ref, v_ref, g_ref, cos_ref, sin_ref, decay_ref, kdec_ref, qdec_ref,
                      o_ref, state_ref, *, chunk_decay):
    @pl.when(pl.program_id(1) == 0)
    def _():
        state_ref[...] = jnp.zeros_like(state_ref)

    cos = cos_ref[...]
    sin = sin_ref[...]
    q = _rotary_block(q_ref[...].astype(F32), cos, sin)
    k = _rotary_block(k_ref[...].astype(F32), cos, sin) * (HEAD_DIM ** -0.5)
    qd = (q * qdec_ref[...]).astype(BF16)
    kd = (k * kdec_ref[...]).astype(BF16)
    qb = q.astype(BF16)
    kb = k.astype(BF16)
    v = v_ref[...]
    g = g_ref[...].astype(F32)
    for h in range(N_HEADS):
        sl = slice(h * HEAD_DIM, (h + 1) * HEAD_DIM)
        s = lax.dot_general(qb[:, sl], kb[:, sl], (((1,), (1,)), ((), ())),
                            preferred_element_type=F32) * decay_ref[h]
        intra = jnp.dot(s.astype(BF16), v[:, sl], preferred_element_type=F32)
        st = state_ref[h]
        inter = jnp.dot(qd[:, sl], st.astype(BF16), preferred_element_type=F32)
        kv = lax.dot_general(kd[:, sl], v[:, sl], (((0,), (0,)), ((), ())),
                             preferred_element_type=F32)
        state_ref[h] = chunk_decay[h] * st + kv
        y = intra + inter
        y = y * lax.rsqrt(jnp.mean(y * y, axis=-1, keepdims=True) + EPS)
        o_ref[:, sl] = (_silu(g[:, sl]) * y).astype(o_ref.dtype)


def _retention(proj, batch, seq):
    t = proj.shape[0]
    n = seq // CHUNK
    cos, sin, decay, kdec, qdec, chunk_decay = _retention_consts(seq)
    col = lambda c: pl.BlockSpec((CHUNK, GROUP_W), lambda b, i, c=c: (b * n + i, c))
    full2 = lambda arr: pl.BlockSpec(arr.shape, lambda b, i: (0,) * arr.ndim)
    return pl.pallas_call(
        functools.partial(_retention_kernel, chunk_decay=chunk_decay),
        grid=(batch, n),
        in_specs=[col(0), col(1), col(2), col(3),
                  pl.BlockSpec((CHUNK, GROUP_W), lambda b, i: (i, 0)),
                  pl.BlockSpec((CHUNK, GROUP_W), lambda b, i: (i, 0)),
                  full2(decay), full2(kdec), full2(qdec)],
        out_specs=pl.BlockSpec((CHUNK, GROUP_W), lambda b, i: (b * n + i, 0)),
        out_shape=jax.ShapeDtypeStruct((t, GROUP_W), BF16),
        scratch_shapes=[pltpu.VMEM((N_HEADS, HEAD_DIM, HEAD_DIM), F32)],
        compiler_params=_cparams("arbitrary", "arbitrary"),
        name="retention",
    )(proj, proj, proj, proj, cos, sin, decay, kdec, qdec)


def _scan_rows(a, b):
    rows = a.shape[0]
    row = lax.broadcasted_iota(jnp.int32, (rows, 1), 0)
    d = 1
    while d < rows:
        ar = pltpu.roll(a, d, axis=0)
        br = pltpu.roll(b, d, axis=0)
        m = row >= d
        b = jnp.where(m, a * br + b, b)
        a = jnp.where(m, a * ar, a)
        d *= 2
    return a, b


def _lru_kernel(lg_ref, lx_ref, cw_ref, cb_ref, wa_ref, ba_ref, wx_ref, bx_ref, lam_ref,
                o_ref, prev_ref, h_ref):
    @pl.when(pl.program_id(1) == 0)
    def _():
        prev_ref[...] = jnp.zeros_like(prev_ref)
        h_ref[...] = jnp.zeros_like(h_ref)

    x = lx_ref[...].astype(F32)
    ts = x.shape[0]
    prev = prev_ref[...]
    row8 = lax.broadcasted_iota(jnp.int32, (SUBLANES, 1), 0)
    xc = x * cw_ref[CONV_W - 1:CONV_W, :] + cb_ref[...]
    for sh in range(1, CONV_W):
        xr = pltpu.roll(x, sh, axis=0)
        pr = pltpu.roll(prev, sh, axis=0)
        top = jnp.where(row8 < sh, pr, xr[:SUBLANES])
        xs = jnp.concatenate([top, xr[SUBLANES:]], axis=0)
        xc = xc + xs * cw_ref[CONV_W - 1 - sh:CONV_W - sh, :]
    prev_ref[...] = x[ts - SUBLANES:]

    xcb = xc.astype(BF16)
    rs, is_ = [], []
    for h in range(N_HEADS):
        sl = slice(h * HEAD_DIM, (h + 1) * HEAD_DIM)
        rs.append(jnp.dot(xcb[:, sl], wa_ref[h].astype(BF16), preferred_element_type=F32))
        is_.append(jnp.dot(xcb[:, sl], wx_ref[h].astype(BF16), preferred_element_type=F32))
    r = jax.nn.sigmoid(jnp.concatenate(rs, axis=1) + ba_ref[...])
    i = jax.nn.sigmoid(jnp.concatenate(is_, axis=1) + bx_ref[...])
    lam = lam_ref[...]
    softplus_neg_lam = jnp.maximum(-lam, 0.0) + jnp.log1p(jnp.exp(-jnp.abs(lam)))
    log_a = -LRU_C * r * softplus_neg_lam
    a = jnp.exp(log_a)
    th = jnp.tanh(log_a)
    b = jnp.sqrt(-2.0 * th / (1.0 - th)) * (i * xc)
    a_cum, hs = _scan_rows(a, b)
    hcur = a_cum * h_ref[SUBLANES - 1:SUBLANES, :] + hs
    h_ref[...] = hcur[ts - SUBLANES:]
    o_ref[...] = (hcur * _gelu_tanh(lg_ref[...].astype(F32))).astype(o_ref.dtype)


def _lru(proj, batch, seq, conv_w, conv_b, wa, ba, wx, bx, lam):
    t = proj.shape[0]
    ts = 256
    n = seq // ts
    row = lambda v: v.reshape(1, GROUP_W)
    vec = pl.BlockSpec((1, GROUP_W), lambda b, i: (0, 0))
    mat = pl.BlockSpec((N_HEADS, HEAD_DIM, HEAD_DIM), lambda b, i: (0, 0, 0))
    return pl.pallas_call(
        _lru_kernel,
        grid=(batch, n),
        in_specs=[pl.BlockSpec((ts, GROUP_W), lambda b, i: (b * n + i, 4)),
                  pl.BlockSpec((ts, GROUP_W), lambda b, i: (b * n + i, 5)),
                  pl.BlockSpec((CONV_W, GROUP_W), lambda b, i: (0, 0)),
                  vec, mat, vec, mat, vec, vec],
        out_specs=pl.BlockSpec((ts, GROUP_W), lambda b, i: (b * n + i, 0)),
        out_shape=jax.ShapeDtypeStruct((t, GROUP_W), BF16),
        scratch_shapes=[pltpu.VMEM((SUBLANES, GROUP_W), F32), pltpu.VMEM((SUBLANES, GROUP_W), F32)],
        compiler_params=_cparams("arbitrary", "arbitrary"),
        name="rg_lru",
    )(proj, proj, conv_w, row(conv_b), wa, row(ba), wx, row(bx), row(lam))


def _sgu_kernel(u_ref, v_ref, ng_ref, w_ref, bias_ref, o_ref):
    v = _gelu_tanh(v_ref[...].astype(F32))
    v = v * lax.rsqrt(jnp.mean(v * v, axis=-1, keepdims=True) + EPS) * ng_ref[...]
    vb = v.astype(BF16)
    u = _gelu_tanh(u_ref[...].astype(F32))
    r = lax.broadcasted_iota(jnp.int32, (CHUNK, CHUNK), 0)
    c = lax.broadcasted_iota(jnp.int32, (CHUNK, CHUNK), 1)
    tril = r >= c
    bias = bias_ref[...]
    for g in range(N_HEADS):
        sl = slice(g * HEAD_DIM, (g + 1) * HEAD_DIM)
        w = jnp.where(tril, w_ref[g], 0.0).astype(BF16)
        for ch in range(u.shape[0] // CHUNK):
            rs = slice(ch * CHUNK, (ch + 1) * CHUNK)
            mixed = jnp.dot(w, vb[rs, sl], preferred_element_type=F32) + bias[:, sl]
            o_ref[rs, sl] = (u[rs, sl] * mixed).astype(o_ref.dtype)


def _sgu(proj, norm_g, w_s, b_s):
    t = proj.shape[0]
    ts = 512
    bias_full = jnp.repeat(b_s.T, HEAD_DIM, axis=1)
    return pl.pallas_call(
        _sgu_kernel,
        grid=(t // ts,),
        in_specs=[pl.BlockSpec((ts, GROUP_W), lambda i: (i, 6)),
                  pl.BlockSpec((ts, GROUP_W), lambda i: (i, 7)),
                  pl.BlockSpec((1, GROUP_W), lambda i: (0, 0)),
                  pl.BlockSpec((N_HEADS, CHUNK, CHUNK), lambda i: (0, 0, 0)),
                  pl.BlockSpec((CHUNK, GROUP_W), lambda i: (0, 0))],
        out_specs=pl.BlockSpec((ts, GROUP_W), lambda i: (i, 0)),
        out_shape=jax.ShapeDtypeStruct((t, GROUP_W), BF16),
        compiler_params=_cparams("arbitrary"),
        name="sgu",
    )(proj, proj, norm_g.reshape(1, GROUP_W), w_s, bias_full)


def _fox_prep_kernel(f_ref, fb_ref, cum_ref, cumt_ref):
    z = f_ref[...] + fb_ref[...]
    logf = jnp.minimum(z, 0.0) - jnp.log1p(jnp.exp(-jnp.abs(z)))
    rows = z.shape[0]
    row = lax.broadcasted_iota(jnp.int32, (rows, 1), 0)
    d = 1
    while d < rows:
        logf = logf + jnp.where(row >= d, pltpu.roll(logf, d, axis=0), 0.0)
        d *= 2
    cum_ref[...] = logf
    cumt_ref[...] = logf.T[:SUBLANES, :]


def _fox_prep(f_logit, fb_pad, batch, seq):
    return pl.pallas_call(
        _fox_prep_kernel,
        grid=(batch,),
        in_specs=[pl.BlockSpec((seq, LANES), lambda b: (b, 0)),
                  pl.BlockSpec((1, LANES), lambda b: (0, 0))],
        out_specs=[pl.BlockSpec((seq, LANES), lambda b: (b, 0)),
                   pl.BlockSpec((SUBLANES, seq), lambda b: (b, 0))],
        out_shape=[jax.ShapeDtypeStruct((batch * seq, LANES), F32),
                   jax.ShapeDtypeStruct((batch * SUBLANES, seq), F32)],
        compiler_params=_cparams("arbitrary"),
        name="fox_prep",
    )(f_logit, fb_pad)


def _qk_norm_kernel(q_ref, k_ref, qn_ref, kn_ref, qo_ref, ko_ref):
    for src, gain, dst, mult in ((q_ref, qn_ref, qo_ref, HEAD_DIM ** -0.5), (k_ref, kn_ref, ko_ref, 1.0)):
        x = src[...].astype(F32)
        for h in range(N_HEADS):
            sl = slice(h * HEAD_DIM, (h + 1) * HEAD_DIM)
            xh = x[:, sl]
            y = xh * lax.rsqrt(jnp.mean(xh * xh, axis=-1, keepdims=True) + EPS) * gain[...]
            dst[:, sl] = (y * mult).astype(dst.dtype)


def _qk_norm(proj, qn, kn):
    t = proj.shape[0]
    ts = 512
    out = jax.ShapeDtypeStruct((t, GROUP_W), BF16)
    return pl.pallas_call(
        _qk_norm_kernel,
        grid=(t // ts,),
        in_specs=[pl.BlockSpec((ts, GROUP_W), lambda i: (i, 8)),
                  pl.BlockSpec((ts, GROUP_W), lambda i: (i, 9)),
                  pl.BlockSpec((1, HEAD_DIM), lambda i: (0, 0)),
                  pl.BlockSpec((1, HEAD_DIM), lambda i: (0, 0))],
        out_specs=[pl.BlockSpec((ts, GROUP_W), lambda i: (i, 0))] * 2,
        out_shape=[out, out],
        compiler_params=_cparams("arbitrary"),
        name="qk_norm",
    )(proj, proj, qn.reshape(1, HEAD_DIM), kn.reshape(1, HEAD_DIM))


def _fox_kernel(q_ref, k_ref, v_ref, cum_ref, cumt_ref, o_ref, *, tq):
    h = pl.program_id(1)
    i = pl.program_id(2)
    q = q_ref[...]
    lane = lax.broadcasted_iota(jnp.int32, (tq, LANES), 1)
    cq = jnp.sum(jnp.where(lane == h, cum_ref[...], 0.0), axis=-1, keepdims=True)
    rowi = lax.broadcasted_iota(jnp.int32, (tq, tq), 0)
    coli = lax.broadcasted_iota(jnp.int32, (tq, tq), 1)

    def body(j, carry):
        m, l, acc = carry
        start = pl.multiple_of(j * tq, tq)
        kb = k_ref[pl.ds(start, tq), :]
        vb = v_ref[pl.ds(start, tq), :]
        ck = cumt_ref[pl.ds(h, 1), pl.ds(start, tq)]
        s = lax.dot_general(q, kb, (((1,), (1,)), ((), ())), preferred_element_type=F32)
        s = s + cq - ck
        s = jnp.where(coli + j * tq <= rowi + i * tq, s, NEG_BIG)
        m_new = jnp.maximum(m, jnp.max(s, axis=-1, keepdims=True))
        alpha = jnp.exp(m - m_new)
        p = jnp.exp(s - m_new)
        l = alpha * l + jnp.sum(p, axis=-1, keepdims=True)
        acc = alpha * acc + jnp.dot(p.astype(BF16), vb, preferred_element_type=F32)
        return m_new, l, acc

    m0 = jnp.full((tq, 1), NEG_BIG, F32)
    l0 = jnp.zeros((tq, 1), F32)
    a0 = jnp.zeros((tq, HEAD_DIM), F32)
    _, l, acc = lax.fori_loop(0, i + 1, body, (m0, l0, a0))
    o_ref[...] = (acc / l).astype(o_ref.dtype)


def _fox(qn, kn, proj, cum, cumt, batch, seq):
    t = qn.shape[0]
    tq = 256
    nq = seq // tq
    return pl.pallas_call(
        functools.partial(_fox_kernel, tq=tq),
        grid=(batch, N_HEADS, nq),
        in_specs=[pl.BlockSpec((tq, HEAD_DIM), lambda b, h, i: (b * nq + i, h)),
                  pl.BlockSpec((seq, HEAD_DIM), lambda b, h, i: (b, h)),
                  pl.BlockSpec((seq, HEAD_DIM), lambda b, h, i: (b, 10 * N_HEADS + h)),
                  pl.BlockSpec((tq, LANES), lambda b, h, i: (b * nq + i, 0)),
                  pl.BlockSpec((SUBLANES, seq), lambda b, h, i: (b, 0))],
        out_specs=pl.BlockSpec((tq, HEAD_DIM), lambda b, h, i: (b * nq + i, h)),
        out_shape=jax.ShapeDtypeStruct((t, GROUP_W), BF16),
        compiler_params=_cparams("arbitrary", "arbitrary", "arbitrary"),
        name="fox_attn",
    )(qn, kn, proj, cum, cumt)


def _out_proj_kernel(y0_ref, y1_ref, y2_ref, y3_ref, w_ref, x_ref, mod_ref, o_ref, wb_ref,
                     *, gate_idx, tiles_per_batch):
    @pl.when(pl.program_id(1) == 0)
    def _():
        wb_ref[...] = w_ref[...].astype(BF16)

    b = pl.program_id(1) // tiles_per_batch
    j = pl.program_id(0)
    tn = o_ref.shape[1]
    acc = jnp.zeros(o_ref.shape, F32)
    for g, y_ref in enumerate((y0_ref, y1_ref, y2_ref, y3_ref)):
        acc = acc + jnp.dot(y_ref[...], wb_ref[g * GROUP_W:(g + 1) * GROUP_W, :],
                            preferred_element_type=F32)
    gate = mod_ref[pl.ds(b * N_MOD + gate_idx, 1), pl.ds(pl.multiple_of(j * tn, tn), tn)]
    o_ref[...] = x_ref[...] + gate * acc


def _out_proj(ys, w_out, layer, x2, mod, seq, gate_idx):
    t, d = x2.shape
    tm, tn = 1024, 512
    yspec = pl.BlockSpec((tm, GROUP_W), lambda j, i: (i, 0))
    return pl.pallas_call(
        functools.partial(_out_proj_kernel, gate_idx=gate_idx, tiles_per_batch=seq // tm),
        grid=(d // tn, t // tm),
        in_specs=[yspec, yspec, yspec, yspec,
                  pl.BlockSpec((None, w_out.shape[1], tn), lambda j, i: (layer, 0, j)),
                  pl.BlockSpec((tm, tn), lambda j, i: (i, j)),
                  pl.BlockSpec(mod.shape, lambda j, i: (0, 0))],
        out_specs=pl.BlockSpec((tm, tn), lambda j, i: (i, j)),
        out_shape=jax.ShapeDtypeStruct((t, d), F32),
        scratch_shapes=[pltpu.VMEM((w_out.shape[1], tn), BF16)],
        compiler_params=_cparams("arbitrary", "arbitrary"),
        name="out_proj",
    )(*ys, w_out, x2, mod)


def _split_bf16(x):
    hi = x.astype(BF16)
    lo = (x - hi.astype(F32)).astype(BF16)
    return hi, lo


def _pack_bf16_pair(lo, hi):
    lo_bits = lax.bitcast_convert_type(lo.astype(BF16).astype(F32), U32) >> 16
    hi_bits = lax.bitcast_convert_type(hi.astype(BF16).astype(F32), U32)
    return hi_bits | lo_bits


def _unpack_bf16_pair(u):
    lo = lax.bitcast_convert_type(u << 16, F32)
    hi = lax.bitcast_convert_type(u & jnp.uint32(0xFFFF0000), F32)
    return lo, hi


def _router_kernel(x_ref, g_ref, mod_ref, rw_ref, rb_ref,
                   h_ref, hp_ref, idx_ref, wts_ref, rank_ref, counts_ref, cnt_ref,
                   *, shift_idx, scale_idx, tiles_per_batch):
    @pl.when(pl.program_id(0) == 0)
    def _():
        cnt_ref[...] = jnp.zeros_like(cnt_ref)

    b = pl.program_id(0) // tiles_per_batch
    x = x_ref[...]
    tm, d = x.shape
    ms = jnp.mean(x * x, axis=-1, keepdims=True)
    y = x * lax.rsqrt(ms + EPS) * g_ref[...]
    h = y * (1.0 + _mod_row(mod_ref, b, scale_idx)) + _mod_row(mod_ref, b, shift_idx)
    h_ref[...] = h.astype(h_ref.dtype)
    hp_ref[...] = _pack_bf16_pair(h[:, :d // 2], h[:, d // 2:])

    h_hi, h_lo = _split_bf16(h)
    w_hi, w_lo = _split_bf16(rw_ref[...])
    logits = (jnp.dot(h_hi, w_hi, preferred_element_type=F32)
              + jnp.dot(h_lo, w_hi, preferred_element_type=F32)
              + jnp.dot(h_hi, w_lo, preferred_element_type=F32))
    scores = jax.nn.sigmoid(logits)
    biased = scores + rb_ref[...]
    lane = lax.broadcasted_iota(jnp.int32, scores.shape, 1)
    firsts, sels = [], []
    picked = jnp.zeros(scores.shape, F32)
    for _ in range(TOP_K):
        mx = jnp.max(biased, axis=-1, keepdims=True)
        first = jnp.min(jnp.where(biased == mx, lane, N_EXPERTS), axis=-1, keepdims=True)
        hit = lane == first
        firsts.append(first)
        sels.append(jnp.sum(jnp.where(hit, scores, 0.0), axis=-1, keepdims=True))
        picked = jnp.where(hit, 1.0, picked)
        biased = jnp.where(hit, -jnp.inf, biased)
    total = sels[0]
    for s in sels[1:]:
        total = total + s
    r = lax.broadcasted_iota(jnp.int32, (tm, tm), 0)
    c = lax.broadcasted_iota(jnp.int32, (tm, tm), 1)
    below = jnp.where(r > c, 1.0, 0.0).astype(BF16)
    pos = jnp.dot(below, picked.astype(BF16), preferred_element_type=F32) + cnt_ref[...]
    for k in range(TOP_K):
        idx_ref[:, k:k + 1] = firsts[k]
        wts_ref[:, k:k + 1] = sels[k] / total * ROUTED_SCALE
        rk = jnp.sum(jnp.where(lane == firsts[k], pos, 0.0), axis=-1, keepdims=True)
        rank_ref[:, k:k + 1] = rk
    cnt = cnt_ref[...] + jnp.sum(picked, axis=0, keepdims=True)
    cnt_ref[...] = cnt
    counts_ref[...] = cnt


def _router(x2, g, mod, router_w, layer, router_bias, seq, shift_idx, scale_idx):
    t, d = x2.shape
    tm = 256
    small = lambda dt: jax.ShapeDtypeStruct((t, TOP_K), dt)
    sspec = pl.BlockSpec((tm, TOP_K), lambda i: (i, 0))
    return pl.pallas_call(
        functools.partial(_router_kernel, shift_idx=shift_idx, scale_idx=scale_idx,
                          tiles_per_batch=seq // tm),
        grid=(t // tm,),
        in_specs=[pl.BlockSpec((tm, d), lambda i: (i, 0)),
                  pl.BlockSpec((1, d), lambda i: (0, 0)),
                  pl.BlockSpec(mod.shape, lambda i: (0, 0)),
                  pl.BlockSpec((None, d, N_EXPERTS), lambda i: (layer, 0, 0)),
                  pl.BlockSpec((1, N_EXPERTS), lambda i: (0, 0))],
        out_specs=[pl.BlockSpec((tm, d), lambda i: (i, 0)),
                   pl.BlockSpec((tm, d // 2), lambda i: (i, 0)),
                   sspec, sspec, sspec,
                   pl.BlockSpec((1, N_EXPERTS), lambda i: (0, 0))],
        out_shape=[jax.ShapeDtypeStruct((t, d), BF16),
                   jax.ShapeDtypeStruct((t, d // 2), U32),
                   small(jnp.int32), small(F32), small(F32),
                   jax.ShapeDtypeStruct((1, N_EXPERTS), F32)],
        scratch_shapes=[pltpu.VMEM((1, N_EXPERTS), F32)],
        compiler_params=_cparams("arbitrary"),
        name="norm_router",
    )(x2, g.reshape(1, d), mod, router_w, router_bias.reshape(1, N_EXPERTS))


def _dest_kernel(idx_ref, rank_ref, starts_ref, dest_ref):
    idx = idx_ref[...]
    rank = rank_ref[...]
    starts = starts_ref[...]
    lane = lax.broadcasted_iota(jnp.int32, (idx.shape[0], N_EXPERTS), 1)
    for k in range(TOP_K):
        st = jnp.sum(jnp.where(lane == idx[:, k:k + 1], starts, 0.0), axis=-1, keepdims=True)
        dest_ref[:, k:k + 1] = (st + rank[:, k:k + 1]).astype(jnp.int32)


def _dest_rows(idx, rank, starts):
    t = idx.shape[0]
    tm = 1024
    sspec = pl.BlockSpec((tm, TOP_K), lambda i: (i, 0))
    return pl.pallas_call(
        _dest_kernel,
        grid=(t // tm,),
        in_specs=[sspec, sspec, pl.BlockSpec((1, N_EXPERTS), lambda i: (0, 0))],
        out_specs=sspec,
        out_shape=jax.ShapeDtypeStruct((t, TOP_K), jnp.int32),
        compiler_params=_cparams("arbitrary"),
        name="moe_dest",
    )(idx, rank, starts)


def _work_items(counts, tm, n_items):
    counts = counts.astype(jnp.int32)
    ends = jnp.cumsum(counts)
    starts = ends - counts
    first_tile = starts // tm
    last_tile = jnp.maximum(ends - 1, 0) // tm
    ntiles = jnp.where(counts > 0, last_tile - first_tile + 1, 0)
    item_end = jnp.cumsum(ntiles)
    item_start = item_end - ntiles
    total = item_end[-1]
    w = jnp.arange(n_items, dtype=jnp.int32)
    wc = jnp.minimum(w, total - 1)
    gid = jnp.searchsorted(item_end, wc, side="right").astype(jnp.int32)
    tid = first_tile[gid] + (wc - item_start[gid])
    lo = jnp.maximum(starts[gid], tid * tm)
    hi = jnp.minimum(ends[gid], (tid + 1) * tm)
    hi = jnp.where(w < total, hi, lo)
    return starts, gid, tid, lo, hi


def _row_copy_wait(src_rows_ref, dst_rows_ref, sem, n):
    pltpu.make_async_copy(src_rows_ref, dst_rows_ref, sem).wait()


def _dispatch_kernel(dest_ref, x_ref, hs_ref, sem):
    tm = x_ref.shape[0]

    def body(i, carry):
        for k in range(TOP_K):
            d = dest_ref[i * TOP_K + k]
            pltpu.make_async_copy(x_ref.at[pl.ds(i, 1), :], hs_ref.at[pl.ds(d, 1), :], sem).start()
        return carry

    lax.fori_loop(0, tm, body, 0)
    for k in range(TOP_K):
        _row_copy_wait(x_ref, hs_ref.at[pl.ds(0, tm), :], sem, tm)


def _dispatch(hp, dest_flat):
    t, wdt = hp.shape
    tm = 256
    return pl.pallas_call(
        _dispatch_kernel,
        grid=(t // tm,),
        in_specs=[pl.BlockSpec((tm * TOP_K,), lambda i: (i,), memory_space=pltpu.SMEM),
                  pl.BlockSpec((tm, wdt), lambda i: (i, 0))],
        out_specs=pl.BlockSpec(memory_space=pl.ANY),
        out_shape=jax.ShapeDtypeStruct((t * TOP_K, wdt), U32),
        scratch_shapes=[pltpu.SemaphoreType.DMA(())],
        compiler_params=_cparams("arbitrary"),
        name="moe_dispatch",
    )(dest_flat, hp)


def _experts_kernel(gid_ref, tid_ref, lo_ref, hi_ref, x_ref, wg_ref, wu_ref, wd_ref, o_ref,
                    wgb_ref, wub_ref, wdb_ref):
    w = pl.program_id(0)
    wp = jnp.maximum(w - 1, 0)
    g = gid_ref[w]
    tile = tid_ref[w]
    new_group = jnp.logical_or(w == 0, g != gid_ref[wp])
    first_visit = jnp.logical_or(w == 0, tile != tid_ref[wp])
    lo = lo_ref[w]
    hi = hi_ref[w]

    @pl.when(new_group)
    def _():
        wgb_ref[...] = wg_ref[...].astype(BF16)
        wub_ref[...] = wu_ref[...].astype(BF16)
        wdb_ref[...] = wd_ref[...].astype(BF16)

    @pl.when(hi > lo)
    def _():
        tm, half = x_ref.shape
        x_lo, x_hi = _unpack_bf16_pair(x_ref[...])
        x_lo = x_lo.astype(BF16)
        x_hi = x_hi.astype(BF16)
        gt = (jnp.dot(x_lo, wgb_ref[:half, :], preferred_element_type=F32)
              + jnp.dot(x_hi, wgb_ref[half:, :], preferred_element_type=F32))
        up = (jnp.dot(x_lo, wub_ref[:half, :], preferred_element_type=F32)
              + jnp.dot(x_hi, wub_ref[half:, :], preferred_element_type=F32))
        hid = (_silu(gt) * up).astype(BF16)
        y = jnp.dot(hid, wdb_ref[...], preferred_element_type=F32)
        packed = _pack_bf16_pair(y[:, :half], y[:, half:])
        rows = tile * tm + lax.broadcasted_iota(jnp.int32, (tm, 1), 0)
        mine = jnp.logical_and(rows >= lo, rows < hi)

        @pl.when(first_visit)
        def _():
            o_ref[...] = jnp.where(mine, packed, jnp.uint32(0))

        @pl.when(jnp.logical_not(first_visit))
        def _():
            o_ref[...] = jnp.where(mine, packed, o_ref[...])


def _experts(hs, w_gate, w_up, w_down, layer, gid, tid, lo, hi, tm):
    p, half = hs.shape
    d, f = w_gate.shape[-2:]
    n_items = gid.shape[0]
    grid_spec = pltpu.PrefetchScalarGridSpec(
        num_scalar_prefetch=4,
        grid=(n_items,),
        in_specs=[pl.BlockSpec((tm, half), lambda w, gid, tid, lo, hi: (tid[w], 0)),
                  pl.BlockSpec((None, None, d, f), lambda w, gid, tid, lo, hi: (layer, gid[w], 0, 0)),
                  pl.BlockSpec((None, None, d, f), lambda w, gid, tid, lo, hi: (layer, gid[w], 0, 0)),
                  pl.BlockSpec((None, None, f, d), lambda w, gid, tid, lo, hi: (layer, gid[w], 0, 0))],
        out_specs=pl.BlockSpec((tm, half), lambda w, gid, tid, lo, hi: (tid[w], 0)),
        scratch_shapes=[pltpu.VMEM((d, f), BF16), pltpu.VMEM((d, f), BF16), pltpu.VMEM((f, d), BF16)],
    )
    return pl.pallas_call(
        _experts_kernel,
        grid_spec=grid_spec,
        out_shape=jax.ShapeDtypeStruct((p, half), U32),
        compiler_params=_cparams("arbitrary"),
        name="moe_experts",
    )(gid, tid, lo, hi, hs, w_gate, w_up, w_down)


def _combine_kernel(dest_ref, ys_ref, wts_ref, shh_ref, swd_ref, x_ref, mod_ref, o_ref,
                    buf_ref, swb_ref, sem, *, gate_idx, tiles_per_batch):
    i = pl.program_id(0)

    @pl.when(i == 0)
    def _():
        swb_ref[...] = swd_ref[...].astype(BF16)

    tm, d = x_ref.shape
    half = d // 2

    def body(r, carry):
        for k in range(TOP_K):
            row = dest_ref[r * TOP_K + k]
            pltpu.make_async_copy(ys_ref.at[pl.ds(row, 1), :], buf_ref.at[k, pl.ds(r, 1), :], sem).start()
        return carry

    lax.fori_loop(0, tm, body, 0)
    shared = jnp.dot(shh_ref[...], swb_ref[...], preferred_element_type=F32)
    for k in range(TOP_K):
        _row_copy_wait(ys_ref.at[pl.ds(0, tm), :], buf_ref.at[k], sem, tm)
    wts = wts_ref[...]
    acc_lo = shared[:, :half]
    acc_hi = shared[:, half:]
    for k in range(TOP_K):
        y_lo, y_hi = _unpack_bf16_pair(buf_ref[k])
        wk = wts[:, k:k + 1]
        acc_lo = acc_lo + wk * y_lo
        acc_hi = acc_hi + wk * y_hi
    b = i // tiles_per_batch
    gate = _mod_row(mod_ref, b, gate_idx)
    o_ref[:, :half] = x_ref[:, :half] + gate[:, :half] * acc_lo
    o_ref[:, half:] = x_ref[:, half:] + gate[:, half:] * acc_hi


def _combine(ys, dest_flat, wts, sh_hidden, sw_down, layer, x2, mod, seq, gate_idx):
    t, d = x2.shape
    f = sh_hidden.shape[1]
    tm = 128
    return pl.pallas_call(
        functools.partial(_combine_kernel, gate_idx=gate_idx, tiles_per_batch=seq // tm),
        grid=(t // tm,),
        in_specs=[pl.BlockSpec((tm * TOP_K,), lambda i: (i,), memory_space=pltpu.SMEM),
                  pl.BlockSpec(memory_space=pl.ANY),
                  pl.BlockSpec((tm, TOP_K), lambda i: (i, 0)),
                  pl.BlockSpec((tm, f), lambda i: (i, 0)),
                  pl.BlockSpec((None, f, d), lambda i: (layer, 0, 0)),
                  pl.BlockSpec((tm, d), lambda i: (i, 0)),
                  pl.BlockSpec(mod.shape, lambda i: (0, 0))],
        out_specs=pl.BlockSpec((tm, d), lambda i: (i, 0)),
        out_shape=jax.ShapeDtypeStruct((t, d), F32),
        scratch_shapes=[pltpu.VMEM((TOP_K, tm, d // 2), U32), pltpu.VMEM((f, d), BF16),
                        pltpu.SemaphoreType.DMA(())],
        compiler_params=_cparams("arbitrary"),
        name="moe_combine",
    )(dest_flat, ys, wts, sh_hidden, sw_down, x2, mod)


def _shared_up_kernel(h_ref, wg_ref, wu_ref, o_ref, wgb_ref, wub_ref):
    @pl.when(pl.program_id(0) == 0)
    def _():
        wgb_ref[...] = wg_ref[...].astype(BF16)
        wub_ref[...] = wu_ref[...].astype(BF16)

    hb = h_ref[...]
    gt = jnp.dot(hb, wgb_ref[...], preferred_element_type=F32)
    up = jnp.dot(hb, wub_ref[...], preferred_element_type=F32)
    o_ref[...] = (_silu(gt) * up).astype(o_ref.dtype)


def _shared_up(h, w_gate, w_up, layer):
    t, d = h.shape
    f = w_gate.shape[-1]
    tm = 1024
    wspec = pl.BlockSpec((None, d, f), lambda i: (layer, 0, 0))
    return pl.pallas_call(
        _shared_up_kernel,
        grid=(t // tm,),
        in_specs=[pl.BlockSpec((tm, d), lambda i: (i, 0)), wspec, wspec],
        out_specs=pl.BlockSpec((tm, f), lambda i: (i, 0)),
        out_shape=jax.ShapeDtypeStruct((t, f), BF16),
        scratch_shapes=[pltpu.VMEM((d, f), BF16), pltpu.VMEM((d, f), BF16)],
        compiler_params=_cparams("arbitrary"),
        name="shared_up",
    )(h, w_gate, w_up)


def kernel(x, c, w_ada, b_ada, ada_table, norm1_g, norm2_g, w_in, w_out, lru_conv_w, lru_conv_b, lru_wa, lru_ba, lru_wx, lru_bx, lru_lambda, sgu_norm_g, sgu_w, sgu_b, fox_qn, fox_kn, fox_fb, router_w, router_bias, exp_w_gate, exp_w_up, exp_w_down, sh_w_gate, sh_w_up, sh_w_down):
    batch, seq, d = x.shape
    depth = ada_table.shape[0]
    t = batch * seq
    bp = SUBLANES * ((batch + SUBLANES - 1) // SUBLANES)
    c_pad = jnp.pad(c, ((0, bp - batch), (0, 0)))
    mod_all = _ada_mod(c_pad, w_ada, b_ada, ada_table)
    x2 = x.reshape(t, d)
    for l in range(depth):
        mod = mod_all[l].reshape(bp * N_MOD, d)
        h = _norm_mod(x2, norm1_g[l], mod, seq, shift_idx=0, scale_idx=1)
        proj = _mm_ws(h, w_in, l, N_PROJ, 512, BF16)
        w_f = jnp.pad(w_in[l, :, N_PROJ:], ((0, 0), (0, LANES - N_HEADS)))
        f_logit = _mm_ws(h, w_f[None], 0, LANES, LANES, F32)
        fb_pad = jnp.pad(fox_fb[l], (0, LANES - N_HEADS)).reshape(1, LANES)
        y_ret = _retention(proj, batch, seq)
        y_lru = _lru(proj, batch, seq, lru_conv_w[l], lru_conv_b[l], lru_wa[l], lru_ba[l],
                     lru_wx[l], lru_bx[l], lru_lambda[l])
        y_sgu = _sgu(proj, sgu_norm_g[l], sgu_w[l], sgu_b[l])
        cum, cumt = _fox_prep(f_logit, fb_pad, batch, seq)
        qn, kn = _qk_norm(proj, fox_qn[l], fox_kn[l])
        y_fox = _fox(qn, kn, proj, cum, cumt, batch, seq)
        x2 = _out_proj((y_ret, y_lru, y_sgu, y_fox), w_out, l, x2, mod, seq, gate_idx=2)
        h2, h2p, idx, wts, rank, counts = _router(x2, norm2_g[l], mod, router_w, l, router_bias[l],
                                                  seq, shift_idx=3, scale_idx=4)
        n_items = (t * TOP_K) // MOE_TILE + N_EXPERTS - 1
        starts, gid, tid, lo, hi = _work_items(counts[0], MOE_TILE, n_items)
        dest = _dest_rows(idx, rank, starts.astype(F32).reshape(1, N_EXPERTS)).reshape(t * TOP_K)
        hs = _dispatch(h2p, dest)
        ys = _experts(hs, exp_w_gate, exp_w_up, exp_w_down, l, gid, tid, lo, hi, MOE_TILE)
        sh_hidden = _shared_up(h2, sh_w_gate, sh_w_up, l)
        x2 = _combine(ys, dest, wts, sh_hidden, sh_w_down, l, x2, mod, seq, gate_idx=5)
    return x2.reshape(batch, seq, d)
```
